```python
import math
import jax, jax.numpy as jnp
from jax import lax
import numpy as np

D_MODEL = 1024
BATCH = 8
SEQ = 4096
DEPTH = 1
DEC_BATCH = 2
DEC_SEQ = 8192
PAST_LEN = 128

S5_WIDTH = D_MODEL // 2
S5_GROUP = 16
S5_GROUPS = S5_WIDTH // S5_GROUP
S5_STATE = 64
MLA_HEADS = 16
QK_NOPE = 64
QK_ROPE = 32
QK_HEAD = QK_NOPE + QK_ROPE
V_HEAD = 64
Q_LORA = 3 * D_MODEL // 8
KV_LORA = D_MODEL // 4
ROPE_BASE = 10000.0
Q_BLOCK = 128
IN_WIDTH = S5_WIDTH + Q_LORA + KV_LORA + QK_ROPE
N_BRANCH = 2
N_EXPERTS = 32
TOP_K = 4
D_FF = D_MODEL
SWIGLU_LIMIT = 7.0
SWIGLU_ALPHA = 1.702
ROW_BLOCK = 256
EPS = 1e-6

kernel_name = "hybrid_s5_mla_moe_encoder"


def rms_norm(x, g):
    xf = x.astype(jnp.float32)
    y = xf * lax.rsqrt(jnp.mean(xf * xf, axis=-1, keepdims=True) + EPS)
    return (y * g.astype(jnp.float32)).astype(x.dtype)


def rope(x, pos):
    half = QK_ROPE // 2
    inv = ROPE_BASE ** (-jnp.arange(half, dtype=jnp.float32) / half)
    ang = pos[:, None] * inv[None, :]
    cos = jnp.cos(ang)[:, None, :]
    sin = jnp.sin(ang)[:, None, :]
    x1 = x[..., :half].astype(jnp.float32)
    x2 = x[..., half:].astype(jnp.float32)
    return jnp.concatenate([x1 * cos - x2 * sin, x1 * sin + x2 * cos], axis=-1).astype(x.dtype)


def _ssm_combine(e1, e2):
    a1, b1 = e1
    a2, b2 = e2
    return a1 * a2, a2 * b1 + b2


def s5_mixer(xs, p):
    f32 = jnp.float32
    lam = lax.complex(p["s5_lam_re"].astype(f32), p["s5_lam_im"].astype(f32))
    dt = jnp.exp(p["s5_log_dt"].astype(f32))[..., None]
    a_bar = jnp.exp(lam * dt)
    b = lax.complex(p["s5_b_re"].astype(f32), p["s5_b_im"].astype(f32))
    b_bar = ((a_bar - 1.0) / lam)[..., None] * b
    c = lax.complex(p["s5_c_re"].astype(f32), p["s5_c_im"].astype(f32))
    d = p["s5_d"].astype(f32)

    def one_sequence(u):
        s_len = u.shape[0]
        ug = u.reshape(s_len, S5_GROUPS, S5_GROUP).astype(jnp.complex64)
        y = d * u
        for direction, rev in ((0, False), (1, True)):
            bu = jnp.einsum("sgh,gnh->sgn", ug, b_bar[direction])
            a = jnp.broadcast_to(a_bar[direction], bu.shape)
            _, h = lax.associative_scan(_ssm_combine, (a, bu), axis=0, reverse=rev)
            y = y + jnp.real(jnp.einsum("sgn,ghn->sgh", h, c[direction])).reshape(s_len, S5_WIDTH)
        return y

    y = lax.map(one_sequence, xs.astype(f32))
    y = jax.nn.gelu(y)
    y = y * jax.nn.sigmoid(y @ p["s5_w_glu"] + p["s5_b_glu"])
    return y.astype(xs.dtype)


def mla_mixer(q_lat, kv_lat, k_rope_in, p, pos):
    bsz, s_len, _ = q_lat.shape
    q = (rms_norm(q_lat, p["g_q_lat"]) @ p["w_q_up"]).reshape(bsz, s_len, MLA_HEADS, QK_HEAD)
    kv = (rms_norm(kv_lat, p["g_kv_lat"]) @ p["w_kv_up"]).reshape(bsz, s_len, MLA_HEADS, QK_NOPE + V_HEAD)
    k_nope, v = kv[..., :QK_NOPE], kv[..., QK_NOPE:]
    k_r = jnp.broadcast_to(k_rope_in[:, :, None, :], (bsz, s_len, MLA_HEADS, QK_ROPE))
    k = jnp.concatenate([k_nope, k_r], axis=-1)
    q = rms_norm(q, p["g_qnorm"])
    k = rms_norm(k, p["g_knorm"])
    q = jnp.concatenate([q[..., :QK_NOPE], rope(q[..., QK_NOPE:], pos)], axis=-1)
    k = jnp.concatenate([k[..., :QK_NOPE], rope(k[..., QK_NOPE:], pos)], axis=-1)
    n_qb = s_len // Q_BLOCK
    qb = q.reshape(bsz, n_qb, Q_BLOCK, MLA_HEADS, QK_HEAD).transpose(1, 0, 2, 3, 4)
    scale = QK_HEAD ** -0.5

    def attend(q_blk):
        s = jnp.einsum("bqhd,bkhd->bhqk", q_blk, k, preferred_element_type=jnp.float32) * scale
        pr = jax.nn.softmax(s, axis=-1).astype(v.dtype)
        return jnp.einsum("bhqk,bkhd->bqhd", pr, v)

    o = lax.map(attend, qb)
    return o.transpose(1, 0, 2, 3, 4).reshape(bsz, s_len, MLA_HEADS * V_HEAD)


def moe_ffn(x, p):
    n_tok = x.shape[0]
    logits = (x @ p["w_router"]).astype(jnp.float32) + p["b_router"].astype(jnp.float32)
    top_val, top_idx = lax.top_k(logits, TOP_K)
    top_w = jax.nn.softmax(top_val, axis=-1)
    n_slots = n_tok * TOP_K
    flat_e = top_idx.reshape(-1)
    flat_tok = jnp.repeat(jnp.arange(n_tok, dtype=jnp.int32), TOP_K)
    flat_w = top_w.reshape(-1)
    order = jnp.argsort(flat_e)
    sorted_e = flat_e[order]
    counts = jnp.zeros((N_EXPERTS,), jnp.int32).at[flat_e].add(1)
    padded = (counts + ROW_BLOCK - 1) // ROW_BLOCK * ROW_BLOCK
    start = jnp.cumsum(counts) - counts
    pad_end = jnp.cumsum(padded)
    pad_start = pad_end - padded
    dest = pad_start[sorted_e] + (jnp.arange(n_slots, dtype=jnp.int32) - start[sorted_e])
    n_rows = (n_slots + ROW_BLOCK - 1) // ROW_BLOCK * ROW_BLOCK + N_EXPERTS * ROW_BLOCK
    n_blocks = n_rows // ROW_BLOCK
    row_tok = jnp.zeros((n_rows,), jnp.int32).at[dest].set(flat_tok[order])
    row_w = jnp.zeros((n_rows,), jnp.float32).at[dest].set(flat_w[order])
    block_e = jnp.minimum(
        jnp.searchsorted(pad_end, jnp.arange(n_blocks, dtype=jnp.int32) * ROW_BLOCK, side="right"),
        N_EXPERTS - 1)
    xs = x[row_tok].reshape(n_blocks, ROW_BLOCK, D_MODEL)
    w_in, b_in = p["w_exp_in"], p["b_exp_in"]
    w_out, b_out = p["w_exp_out"], p["b_exp_out"]

    def expert_block(args):
        xb, e = args
        h = xb @ w_in[e] + b_in[e]
        gate = jnp.minimum(h[:, :D_FF], SWIGLU_LIMIT)
        up = jnp.clip(h[:, D_FF:], -SWIGLU_LIMIT, SWIGLU_LIMIT)
        glu = gate * jax.nn.sigmoid(SWIGLU_ALPHA * gate)
        return ((up + 1.0) * glu) @ w_out[e] + b_out[e]

    out = lax.map(expert_block, (xs, block_e)).reshape(n_rows, D_MODEL)
    return jax.ops.segment_sum(out * row_w[:, None].astype(out.dtype), row_tok, num_segments=n_tok)


def encoder_layer(x, c, p):
    bsz, s_len, _ = x.shape
    pos = jnp.arange(s_len, dtype=jnp.float32)
    mod = (jax.nn.silu(c) @ p["w_ada"] + p["b_ada"])[:, None, :]
    shift_m, scale_m, gate_m, shift_f, scale_f, gate_f = jnp.split(mod, 6, axis=-1)
    u = rms_norm(x, p["g_mix"]) * (1.0 + scale_m) + shift_m
    proj = u @ p["w_in"]
    o1 = S5_WIDTH
    o2 = o1 + Q_LORA
    o3 = o2 + KV_LORA
    br_s5 = s5_mixer(proj[..., :o1], p) @ p["w_br_s5"]
    br_mla = mla_mixer(proj[..., o1:o2], proj[..., o2:o3], proj[..., o3:], p, pos) @ p["w_br_mla"]
    g_s5, g_mla = jnp.split(jax.nn.sigmoid(u @ p["w_gate"] + p["b_gate"]), N_BRANCH, axis=-1)
    mixed = (g_s5 * br_s5 + g_mla * br_mla) @ p["w_out"]
    x = x + gate_m * mixed
    v = rms_norm(x, p["g_ffn"]) * (1.0 + scale_f) + shift_f
    y = moe_ffn(v.reshape(bsz * s_len, D_MODEL), p).reshape(bsz, s_len, D_MODEL)
    return x + gate_f * y


def setup_inputs(seed: int = 0) -> dict:
    key = jax.random.key(seed)
    ks = jax.random.split(key, 40)
    L, G, N = DEPTH, S5_GROUPS, S5_STATE

    def nrm(k, shape, scale):
        return jax.random.normal(k, shape, jnp.float32) * scale

    def gain(k, shape):
        return 1.0 + 0.05 * jax.random.normal(k, shape, jnp.float32)

    lam_im = jnp.pi * jnp.arange(N, dtype=jnp.float32) + 0.01 * jax.random.normal(ks[9], (L, 2, G, N), jnp.float32)
    return {
        "x_prompt": nrm(ks[0], (BATCH, SEQ, D_MODEL), 1.0),
        "x_sample": nrm(ks[1], (DEC_BATCH, DEC_SEQ, D_MODEL), 1.0),
        "c_prompt": nrm(ks[2], (BATCH, D_MODEL), 1.0),
        "c_sample": nrm(ks[3], (DEC_BATCH, D_MODEL), 1.0),
        "w_ada": nrm(ks[4], (L, D_MODEL, 6 * D_MODEL), 0.5 * D_MODEL ** -0.5),
        "b_ada": nrm(ks[5], (L, 6 * D_MODEL), 0.01),
        "g_mix": gain(ks[6], (L, D_MODEL)),
        "g_ffn": gain(ks[7], (L, D_MODEL)),
        "w_in": nrm(ks[8], (L, D_MODEL, IN_WIDTH), D_MODEL ** -0.5),
        "s5_lam_re": -0.5 + 0.01 * jax.random.normal(ks[10], (L, 2, G, N), jnp.float32),
        "s5_lam_im": lam_im,
        "s5_log_dt": jax.random.uniform(ks[11], (L, 2, G), jnp.float32, math.log(1e-3), math.log(1e-1)),
        "s5_b_re": nrm(ks[12], (L, 2, G, N, S5_GROUP), (2 * S5_GROUP) ** -0.5),
        "s5_b_im": nrm(ks[13], (L, 2, G, N, S5_GROUP), (2 * S5_GROUP) ** -0.5),
        "s5_c_re": nrm(ks[14], (L, 2, G, S5_GROUP, N), (2 * N) ** -0.5),
        "s5_c_im": nrm(ks[15], (L, 2, G, S5_GROUP, N), (2 * N) ** -0.5),
        "s5_d": nrm(ks[16], (L, S5_WIDTH), 1.0),
        "s5_w_glu": nrm(ks[17], (L, S5_WIDTH, S5_WIDTH), S5_WIDTH ** -0.5),
        "s5_b_glu": nrm(ks[18], (L, S5_WIDTH), 0.01),
        "w_br_s5": nrm(ks[19], (L, S5_WIDTH, D_MODEL), S5_WIDTH ** -0.5),
        "g_q_lat": gain(ks[20], (L, Q_LORA)),
        "w_q_up": nrm(ks[21], (L, Q_LORA, MLA_HEADS * QK_HEAD), Q_LORA ** -0.5),
        "g_kv_lat": gain(ks[22], (L, KV_LORA)),
        "w_kv_up": nrm(ks[23], (L, KV_LORA, MLA_HEADS * (QK_NOPE + V_HEAD)), KV_LORA ** -0.5),
        "g_qnorm": gain(ks[24], (L, QK_HEAD)),
        "g_knorm": gain(ks[25], (L, QK_HEAD)),
        "w_br_mla": nrm(ks[26], (L, MLA_HEADS * V_HEAD, D_MODEL), (MLA_HEADS * V_HEAD) ** -0.5),
        "w_gate": nrm(ks[27], (L, D_MODEL, N_BRANCH * D_MODEL), D_MODEL ** -0.5),
        "b_gate": nrm(ks[28], (L, N_BRANCH * D_MODEL), 0.01),
        "w_out": nrm(ks[29], (L, D_MODEL, D_MODEL), D_MODEL ** -0.5),
        "w_router": nrm(ks[30], (L, D_MODEL, N_EXPERTS), D_MODEL ** -0.5),
        "b_router": nrm(ks[31], (L, N_EXPERTS), 0.01),
        "w_exp_in": nrm(ks[32], (L, N_EXPERTS, D_MODEL, 2 * D_FF), D_MODEL ** -0.5),
        "b_exp_in": nrm(ks[33], (L, N_EXPERTS, 2 * D_FF), 0.01),
        "w_exp_out": nrm(ks[34], (L, N_EXPERTS, D_FF, D_MODEL), D_FF ** -0.5),
        "b_exp_out": nrm(ks[35], (L, N_EXPERTS, D_MODEL), 0.01),
    }


def reference(x_prompt, x_sample, c_prompt, c_sample, w_ada, b_ada, g_mix, g_ffn, w_in,
              s5_lam_re, s5_lam_im, s5_log_dt, s5_b_re, s5_b_im, s5_c_re, s5_c_im, s5_d,
              s5_w_glu, s5_b_glu, w_br_s5, g_q_lat, w_q_up, g_kv_lat, w_kv_up, g_qnorm, g_knorm,
              w_br_mla, w_gate, b_gate, w_out, w_router, b_router, w_exp_in, b_exp_in,
              w_exp_out, b_exp_out):
    y_prompt = x_prompt
    y_sample = x_sample
    for l in range(DEPTH):
        p = {
            "w_ada": w_ada[l], "b_ada": b_ada[l], "g_mix": g_mix[l], "g_ffn": g_ffn[l],
            "w_in": w_in[l],
            "s5_lam_re": s5_lam_re[l], "s5_lam_im": s5_lam_im[l], "s5_log_dt": s5_log_dt[l],
            "s5_b_re": s5_b_re[l], "s5_b_im": s5_b_im[l], "s5_c_re": s5_c_re[l], "s5_c_im": s5_c_im[l],
            "s5_d": s5_d[l], "s5_w_glu": s5_w_glu[l], "s5_b_glu": s5_b_glu[l], "w_br_s5": w_br_s5[l],
            "g_q_lat": g_q_lat[l], "w_q_up": w_q_up[l], "g_kv_lat": g_kv_lat[l], "w_kv_up": w_kv_up[l],
            "g_qnorm": g_qnorm[l], "g_knorm": g_knorm[l], "w_br_mla": w_br_mla[l],
            "w_gate": w_gate[l], "b_gate": b_gate[l], "w_out": w_out[l],
            "w_router": w_router[l], "b_router": b_router[l],
            "w_exp_in": w_exp_in[l], "b_exp_in": b_exp_in[l],
            "w_exp_out": w_exp_out[l], "b_exp_out": b_exp_out[l],
        }
        y_prompt = encoder_layer(y_prompt, c_prompt, p)
        y_sample = encoder_layer(y_sample, c_sample, p)
    return (y_prompt, y_sample)
```

```python
import functools
import math

import jax
import jax.numpy as jnp
import numpy as np
from jax import lax
from jax.experimental import pallas as pl
from jax.experimental.pallas import tpu as pltpu

F32 = jnp.float32
BF16 = jnp.bfloat16
U32 = jnp.uint32
I32 = jnp.int32

EPS = 1e-6
S5_GROUP = 16
S5_STATE = 64
HEADS = 16
QK_NOPE = 64
QK_ROPE = 32
QK_HEAD = QK_NOPE + QK_ROPE
V_HEAD = 64
ROPE_BASE = 10000.0
TOP_K = 4
SWIGLU_LIMIT = 7.0
SWIGLU_ALPHA = 1.702
N_MOD = 6

LANES = 128
VMEM_LIMIT_BYTES = 56 * 1024 * 1024

TOKEN_TILE = 512
QKV_TILE = 256
ATTN_Q_TILE = 256
EXPERT_ROWS = 512
DISPATCH_TOKENS = 512
COMBINE_TOKENS = 128
NEG_BIG = -1e30

_NT = (((1,), (1,)), ((), ()))


def _params(*sem):
    return pltpu.CompilerParams(dimension_semantics=sem, vmem_limit_bytes=VMEM_LIMIT_BYTES)


def _rms(x, g, n=None):
    n = x.shape[-1] if n is None else n
    ms = jnp.sum(x * x, axis=-1, keepdims=True) * (1.0 / n)
    return x * lax.rsqrt(ms + EPS) * g


def _pack_bf16_pairs(x):
    k = x.shape[1] // 2
    bits = pltpu.bitcast(x.astype(BF16).astype(F32), U32)
    return (bits[:, :k] & jnp.uint32(0xFFFF0000)) | (bits[:, k:] >> 16)


def _unpack_bf16_pairs(p):
    hi = pltpu.bitcast(p & jnp.uint32(0xFFFF0000), F32)
    lo = pltpu.bitcast(p << 16, F32)
    return hi, lo


def _mod_kernel(c_ref, w_ref, b_ref, o_ref):
    c = c_ref[...]
    s = c * jax.nn.sigmoid(c)
    o_ref[...] = jnp.dot(s, w_ref[...], preferred_element_type=F32,
                         precision=lax.Precision.HIGHEST) + b_ref[...]


def _modulation(c, w_ada, b_ada):
    nb, d = c.shape
    n = w_ada.shape[1]
    tn = 512
    return pl.pallas_call(
        _mod_kernel,
        grid=(n // tn,),
        in_specs=[pl.BlockSpec((nb, d), lambda j: (0, 0)),
                  pl.BlockSpec((d, tn), lambda j: (0, j)),
                  pl.BlockSpec((1, tn), lambda j: (0, j))],
        out_specs=pl.BlockSpec((nb, tn), lambda j: (0, j)),
        out_shape=jax.ShapeDtypeStruct((nb, n), F32),
        compiler_params=_params("arbitrary"),
    )(c, w_ada, b_ada.reshape(1, n))


def _pre_kernel(tb_ref, x_ref, mod_ref, g_ref, ws5t_ref, wlat_ref, wgate_ref, bgate_ref,
                xs5t_ref, lat_ref, gates_ref):
    x = x_ref[...]
    shift = mod_ref[0, 0:1, :]
    scale = mod_ref[0, 1:2, :]
    u = (_rms(x, g_ref[...]) * (1.0 + scale) + shift).astype(BF16)
    xs5t_ref[...] = lax.dot_general(ws5t_ref[...], u, _NT, preferred_element_type=F32).astype(BF16)
    lat_ref[...] = jnp.dot(u, wlat_ref[...], preferred_element_type=F32).astype(BF16)
    z = jnp.dot(u, wgate_ref[...], preferred_element_type=F32) + bgate_ref[...]
    gates_ref[...] = jax.nn.sigmoid(z).astype(BF16)


def _pre_mixer(x, mod3, tile_b, g_mix, ws5t, wlat, wgate, bgate):
    t, d = x.shape
    tm = TOKEN_TILE
    c5, nlat, ng = ws5t.shape[0], wlat.shape[1], wgate.shape[1]
    const = lambda i, tb: (0, 0)
    return pl.pallas_call(
        _pre_kernel,
        grid_spec=pltpu.PrefetchScalarGridSpec(
            num_scalar_prefetch=1,
            grid=(t // tm,),
            in_specs=[pl.BlockSpec((tm, d), lambda i, tb: (i, 0)),
                      pl.BlockSpec((1, N_MOD, d), lambda i, tb: (tb[i], 0, 0)),
                      pl.BlockSpec((1, d), const),
                      pl.BlockSpec((c5, d), const),
                      pl.BlockSpec((d, nlat), const),
                      pl.BlockSpec((d, ng), const),
                      pl.BlockSpec((1, ng), const)],
            out_specs=[pl.BlockSpec((c5, tm), lambda i, tb: (0, i)),
                       pl.BlockSpec((tm, nlat), lambda i, tb: (i, 0)),
                       pl.BlockSpec((tm, ng), lambda i, tb: (i, 0))]),
        out_shape=[jax.ShapeDtypeStruct((c5, t), BF16),
                   jax.ShapeDtypeStruct((t, nlat), BF16),
                   jax.ShapeDtypeStruct((t, ng), BF16)],
        compiler_params=_params("arbitrary"),
    )(tile_b, x, mod3, g_mix, ws5t, wlat, wgate, bgate)


def _s5_tables(lam_re, lam_im, log_dt, b_re, b_im, c_re, c_im, d, n_steps):
    L = LANES
    hp = lax.Precision.HIGHEST
    lam = lax.complex(lam_re.astype(F32), lam_im.astype(F32))
    dt = jnp.exp(log_dt.astype(F32))[..., None]
    ldt = lam * dt
    a_bar = jnp.exp(ldt)
    bb = ((a_bar - 1.0) / lam)[..., None] * lax.complex(b_re.astype(F32), b_im.astype(F32))
    cc = lax.complex(c_re.astype(F32), c_im.astype(F32))
    g = lam.shape[1]
    tau = jnp.arange(L + 1, dtype=F32)
    apow = jnp.exp(ldt[..., None] * tau)
    kern = jnp.real(jnp.einsum("dgpn,dgnt,dgnh->dgtph", cc, apow[..., :L], bb, precision=hp))
    s_idx = jnp.arange(L)[:, None]
    t_idx = jnp.arange(L)[None, :]
    lag_f = jnp.clip(t_idx - s_idx, 0, L - 1)
    lag_b = jnp.clip(s_idx - t_idx, 0, L - 1)
    kf = jnp.where((t_idx >= s_idx)[None, :, :, None, None], kern[0][:, lag_f], 0.0)
    kb = jnp.where((s_idx >= t_idx)[None, :, :, None, None], kern[1][:, lag_b], 0.0)
    tmat = (kf + kb).transpose(0, 4, 1, 3, 2).reshape(g, S5_GROUP * L, S5_GROUP * L)
    pf = apow[0][..., L - 1 - jnp.arange(L)][..., None] * bb[0][:, :, None, :]
    pb = apow[1][..., jnp.arange(L)][..., None] * bb[1][:, :, None, :]

    def p_cols(pc):
        pc = pc.transpose(0, 3, 2, 1).reshape(g, S5_GROUP * L, S5_STATE)
        return jnp.concatenate([jnp.real(pc), jnp.imag(pc)], axis=-1)

    pmat = jnp.concatenate([p_cols(pf), p_cols(pb)], axis=-1)
    qf = cc[0][..., None] * apow[0][:, None, :, 1 + jnp.arange(L)]
    qb = cc[1][..., None] * apow[1][:, None, :, L - jnp.arange(L)]

    def q_rows(qc):
        qc = qc.transpose(0, 2, 1, 3).reshape(g, S5_STATE, S5_GROUP * L)
        return jnp.concatenate([jnp.real(qc), -jnp.imag(qc)], axis=1)

    qmat = jnp.concatenate([q_rows(qf), q_rows(qb)], axis=1)
    steps = (L * (2 ** jnp.arange(n_steps))).astype(F32)
    al = jnp.exp(ldt[..., None] * steps)
    al = al.transpose(0, 1, 3, 2)
    ar2 = jnp.concatenate([jnp.real(al), jnp.real(al)], axis=-1)
    ai2 = jnp.concatenate([-jnp.imag(al), jnp.imag(al)], axis=-1)
    apw = jnp.concatenate([ar2[0], ai2[0], ar2[1], ai2[1]], axis=1)
    drow = jnp.repeat(d.astype(F32).reshape(g, S5_GROUP), L, axis=1).reshape(g, 1, S5_GROUP * L)
    return tmat.astype(BF16), pmat.astype(BF16), qmat.astype(BF16), apw, drow


def _s5_kernel(pos_ref, xt_ref, t_ref, p_ref, q_ref, apw_ref, d_ref, o_ref, *, n_steps):
    nc = xt_ref.shape[1]
    ns2 = 2 * S5_STATE
    u = jnp.concatenate([xt_ref[h] for h in range(S5_GROUP)], axis=-1)
    z = jnp.dot(u, p_ref[0], preferred_element_type=F32)
    pos = pos_ref[...]

    def scan(zd, base, posd, reverse):
        h = zd
        for k in range(n_steps):
            sh = 1 << k
            hs = pltpu.roll(h, (nc - sh) if reverse else sh, axis=0)
            hsw = pltpu.roll(hs, S5_STATE, axis=1)
            ar = apw_ref[0, base + k:base + k + 1, :]
            ai = apw_ref[0, base + n_steps + k:base + n_steps + k + 1, :]
            h = h + jnp.where(posd >= sh, ar * hs + ai * hsw, 0.0)
        hin = pltpu.roll(h, (nc - 1) if reverse else 1, axis=0)
        return jnp.where(posd >= 1, hin, 0.0)

    hin_f = scan(z[:, :ns2], 0, pos[:, 0:1], False)
    hin_b = scan(z[:, ns2:], 2 * n_steps, pos[:, 1:2], True)
    hin = jnp.concatenate([hin_f, hin_b], axis=-1).astype(BF16)
    y = jnp.dot(u, t_ref[0], preferred_element_type=F32)
    y = y + jnp.dot(hin, q_ref[0], preferred_element_type=F32)
    y = y + d_ref[0] * u.astype(F32)
    y = jax.nn.gelu(y, approximate=True)
    for h in range(S5_GROUP):
        o_ref[h] = y[:, h * LANES:(h + 1) * LANES].astype(o_ref.dtype)


def _s5_mixer(xs5t, chunk_pos, tables, n_steps):
    tmat, pmat, qmat, apw, drow = tables
    c5, t = xs5t.shape
    nc = t // LANES
    g = c5 // S5_GROUP
    gl = S5_GROUP * LANES
    x3 = xs5t.reshape(c5, nc, LANES)
    y3 = pl.pallas_call(
        functools.partial(_s5_kernel, n_steps=n_steps),
        grid=(g,),
        in_specs=[pl.BlockSpec((nc, 2), lambda i: (0, 0)),
                  pl.BlockSpec((S5_GROUP, nc, LANES), lambda i: (i, 0, 0)),
                  pl.BlockSpec((1, gl, gl), lambda i: (i, 0, 0)),
                  pl.BlockSpec((1, gl, 4 * S5_STATE), lambda i: (i, 0, 0)),
                  pl.BlockSpec((1, 4 * S5_STATE, gl), lambda i: (i, 0, 0)),
                  pl.BlockSpec((1, 4 * n_steps, 2 * S5_STATE), lambda i: (i, 0, 0)),
                  pl.BlockSpec((1, 1, gl), lambda i: (i, 0, 0))],
        out_specs=pl.BlockSpec((S5_GROUP, nc, LANES), lambda i: (i, 0, 0)),
        out_shape=jax.ShapeDtypeStruct((c5, nc, LANES), BF16),
        compiler_params=_params("arbitrary"),
    )(chunk_pos, x3, tmat, pmat, qmat, apw, drow)
    return y3.reshape(c5, t)


def _qkv_kernel(pb_ref, lat_ref, gq_ref, gkv_ref, wq_ref, wk_ref, wv_ref, vone_ref, gqn_ref, gkn_ref,
                cos_ref, s1_ref, s2_ref, q_out, k_out, v_out, *, q_lora, kv_lora):
    lat = lat_ref[...].astype(F32)
    qn = _rms(lat[:, :q_lora], gq_ref[...]).astype(BF16)
    kvn = _rms(lat[:, q_lora:q_lora + kv_lora], gkv_ref[...]).astype(BF16)
    kr = lat[:, q_lora + kv_lora:]
    q = jnp.dot(qn, wq_ref[...], preferred_element_type=F32)
    kk = jnp.dot(kvn, wk_ref[...], preferred_element_type=F32)
    v = jnp.dot(kvn, wv_ref[...], preferred_element_type=F32) + vone_ref[...]
    v_out[...] = v.astype(BF16)
    cos, s1, s2 = cos_ref[...], s1_ref[...], s2_ref[...]
    scale = QK_HEAD ** -0.5

    def rope(xh):
        return (xh * cos + pltpu.roll(xh, LANES - QK_ROPE // 2, axis=1) * s1
                + pltpu.roll(xh, QK_ROPE // 2, axis=1) * s2)

    for h in range(HEADS):
        sl = slice(h * LANES, (h + 1) * LANES)
        qh = rope(_rms(q[:, sl], gqn_ref[...], QK_HEAD)) * scale
        q_out[:, sl] = qh.astype(BF16)
        kh = rope(_rms(kk[:, sl] + kr, gkn_ref[...], QK_HEAD))
        k_out[:, sl] = kh.astype(BF16)


def _qkv_prep(lat, tile_pb, g_q_lat, g_kv_lat, wq, wk, wv, vone, gqn, gkn, cos_t, s1_t, s2_t):
    t, nlat = lat.shape
    tm = QKV_TILE
    q_lora, kv_lora = wq.shape[0], wk.shape[0]
    hw = HEADS * LANES
    const = lambda i, pb: (0, 0)
    row = lambda i, pb: (i, 0)
    tab = lambda i, pb: (pb[i], 0)
    kern = functools.partial(_qkv_kernel, q_lora=q_lora, kv_lora=kv_lora)
    return pl.pallas_call(
        kern,
        grid_spec=pltpu.PrefetchScalarGridSpec(
            num_scalar_prefetch=1,
            grid=(t // tm,),
            in_specs=[pl.BlockSpec((tm, nlat), row),
                      pl.BlockSpec((1, q_lora), const),
                      pl.BlockSpec((1, kv_lora), const),
                      pl.BlockSpec((q_lora, hw), const),
                      pl.BlockSpec((kv_lora, hw), const),
                      pl.BlockSpec((kv_lora, hw), const),
                      pl.BlockSpec((1, hw), const),
                      pl.BlockSpec((1, LANES), const),
                      pl.BlockSpec((1, LANES), const),
                      pl.BlockSpec((tm, LANES), tab),
                      pl.BlockSpec((tm, LANES), tab),
                      pl.BlockSpec((tm, LANES), tab)],
            out_specs=[pl.BlockSpec((tm, hw), row)] * 3),
        out_shape=[jax.ShapeDtypeStruct((t, hw), BF16)] * 3,
        compiler_params=_params("arbitrary"),
    )(tile_pb, lat, g_q_lat, g_kv_lat, wq, wk, wv, vone, gqn, gkn, cos_t, s1_t, s2_t)


def _attn_kernel(q_ref, k_ref, v_ref, o_ref):
    outs = []
    for j in range(2):
        sl = slice(j * LANES, (j + 1) * LANES)
        s = lax.dot_general(q_ref[:, sl], k_ref[:, sl], _NT, preferred_element_type=F32)
        m = jnp.max(s, axis=-1, keepdims=True)
        p = jnp.exp(s - m).astype(BF16)
        outs.append(jnp.dot(p, v_ref[:, sl], preferred_element_type=F32))
    lane = lax.broadcasted_iota(I32, outs[0].shape, 1)
    even = outs[0] / outs[0][:, V_HEAD:V_HEAD + 1]
    odd = outs[1] / outs[1][:, 0:1]
    o_ref[...] = jnp.where(lane < V_HEAD, even, odd).astype(o_ref.dtype)


def _attention(q, k, v, seq_groups):
    outs = []
    for (nb, s_len, off) in seq_groups:
        tq = min(ATTN_Q_TILE, s_len)
        assert off % s_len == 0 and s_len % tq == 0
        nq = s_len // tq
        qoff, koff = off // tq, off // s_len
        outs.append(pl.pallas_call(
            _attn_kernel,
            grid=(nb, HEADS // 2, nq),
            in_specs=[pl.BlockSpec((tq, 2 * LANES), lambda b, h, i, nq=nq, qoff=qoff: (qoff + b * nq + i, h)),
                      pl.BlockSpec((s_len, 2 * LANES), lambda b, h, i, koff=koff: (koff + b, h)),
                      pl.BlockSpec((s_len, 2 * LANES), lambda b, h, i, koff=koff: (koff + b, h))],
            out_specs=pl.BlockSpec((tq, LANES), lambda b, h, i, nq=nq: (b * nq + i, h)),
            out_shape=jax.ShapeDtypeStruct((nb * s_len, HEADS * V_HEAD), BF16),
            compiler_params=_params("arbitrary", "arbitrary", "arbitrary"),
        )(q, k, v))
    return jnp.concatenate(outs, axis=0) if len(outs) > 1 else outs[0]


def _mix_kernel(tb_ref, yt_ref, o_ref, gates_ref, x_ref, mod_ref, wglut_ref, bglu_ref, wbs5_ref, wbmla_ref,
                wout_ref, gffn_ref, wr_ref, br_ref, tri_ref,
                x1_out, vp_out, ri_out, rw_out, cnt_out, base_sc):
    i = pl.program_id(0)

    @pl.when(i == 0)
    def _():
        base_sc[...] = jnp.zeros_like(base_sc)

    d = x_ref.shape[1]
    yt = yt_ref[...]
    z = jnp.dot(wglut_ref[...], yt, preferred_element_type=F32) + bglu_ref[...]
    y2 = (yt.astype(F32) * jax.nn.sigmoid(z)).T.astype(BF16)
    br_s5 = jnp.dot(y2, wbs5_ref[...], preferred_element_type=F32)
    br_mla = jnp.dot(o_ref[...], wbmla_ref[...], preferred_element_type=F32)
    g = gates_ref[...].astype(F32)
    merged = (g[:, :d] * br_s5 + g[:, d:] * br_mla).astype(BF16)
    mixed = jnp.dot(merged, wout_ref[...], preferred_element_type=F32)
    gate_m = mod_ref[0, 2:3, :]
    shift_f = mod_ref[0, 3:4, :]
    scale_f = mod_ref[0, 4:5, :]
    x1 = x_ref[...] + gate_m * mixed
    x1_out[...] = x1
    v = _rms(x1, gffn_ref[...]) * (1.0 + scale_f) + shift_f
    vp_out[...] = _pack_bf16_pairs(v)
    logits = jnp.dot(v.astype(BF16), wr_ref[...], preferred_element_type=F32) + br_ref[...]
    lane = lax.broadcasted_iota(I32, logits.shape, 1)
    lane_f = lane.astype(F32)
    work = logits
    mask = jnp.zeros(logits.shape, F32)
    vals, idxs = [], []
    for _ in range(TOP_K):
        mk = jnp.max(work, axis=-1, keepdims=True)
        ik = jnp.min(jnp.where(work == mk, lane_f, float(LANES)), axis=-1, keepdims=True)
        hit = lane_f == ik
        vals.append(mk)
        idxs.append(ik)
        work = jnp.where(hit, NEG_BIG * 2, work)
        mask = mask + hit.astype(F32)
    es = [jnp.exp(vk - vals[0]) for vk in vals]
    den = es[0]
    for e in es[1:]:
        den = den + e
    ws = [e / den for e in es]
    cnt = base_sc[...] + jnp.dot(tri_ref[...], mask.astype(BF16), preferred_element_type=F32)
    ranks = [jnp.sum(jnp.where(lane_f == ik, cnt, 0.0), axis=-1, keepdims=True) for ik in idxs]
    base_sc[...] = base_sc[...] + jnp.sum(mask, axis=0, keepdims=True)
    cnt_out[...] = base_sc[...]
    ri = jnp.zeros(logits.shape, I32)
    rw = jnp.zeros(logits.shape, F32)
    for kk in range(TOP_K):
        ri = jnp.where(lane == kk, idxs[kk].astype(I32), ri)
        ri = jnp.where(lane == TOP_K + kk, ranks[kk].astype(I32), ri)
        rw = jnp.where(lane == kk, ws[kk], rw)
    ri_out[...] = ri
    rw_out[...] = rw


def _mix_route(tile_b, yt, o, gates, x, mod3, wglut, bglu, wbs5, wbmla, wout, gffn, wr, br):
    t, d = x.shape
    tm = TOKEN_TILE
    c5 = yt.shape[0]
    tri = jnp.tril(jnp.ones((tm, tm), F32), -1).astype(BF16)
    const = lambda i, tb: (0, 0)
    row = lambda i, tb: (i, 0)
    return pl.pallas_call(
        _mix_kernel,
        grid_spec=pltpu.PrefetchScalarGridSpec(
            num_scalar_prefetch=1,
            grid=(t // tm,),
            in_specs=[pl.BlockSpec((c5, tm), lambda i, tb: (0, i)),
                      pl.BlockSpec((tm, o.shape[1]), row),
                      pl.BlockSpec((tm, gates.shape[1]), row),
                      pl.BlockSpec((tm, d), row),
                      pl.BlockSpec((1, N_MOD, d), lambda i, tb: (tb[i], 0, 0)),
                      pl.BlockSpec((c5, c5), const),
                      pl.BlockSpec((c5, 1), const),
                      pl.BlockSpec((c5, d), const),
                      pl.BlockSpec(wbmla.shape, const),
                      pl.BlockSpec((d, d), const),
                      pl.BlockSpec((1, d), const),
                      pl.BlockSpec((d, LANES), const),
                      pl.BlockSpec((1, LANES), const),
                      pl.BlockSpec((tm, tm), const)],
            out_specs=[pl.BlockSpec((tm, d), row),
                       pl.BlockSpec((tm, d // 2), row),
                       pl.BlockSpec((tm, LANES), row),
                       pl.BlockSpec((tm, LANES), row),
                       pl.BlockSpec((1, LANES), const)],
            scratch_shapes=[pltpu.VMEM((1, LANES), F32)]),
        out_shape=[jax.ShapeDtypeStruct((t, d), F32),
                   jax.ShapeDtypeStruct((t, d // 2), U32),
                   jax.ShapeDtypeStruct((t, LANES), I32),
                   jax.ShapeDtypeStruct((t, LANES), F32),
                   jax.ShapeDtypeStruct((1, LANES), F32)],
        compiler_params=_params("arbitrary"),
    )(tile_b, yt, o, gates, x, mod3, wglut, bglu, wbs5, wbmla, wout, gffn, wr, br, tri)


def _row_copy(src, dst, s, d, sem):
    return pltpu.make_async_copy(src.at[pl.ds(s, 1)], dst.at[pl.ds(d, 1)], sem)


def _dispatch_kernel(dest_ref, v_hbm, xs_in, xs_out, sem, *, tb):
    del xs_in
    base = pl.program_id(0) * tb

    def issue(j, c):
        for kk in range(TOP_K):
            _row_copy(v_hbm, xs_out, base + j, dest_ref[j * TOP_K + kk], sem).start()
        return c

    lax.fori_loop(0, tb, issue, 0)

    def drain(j, c):
        _row_copy(v_hbm, xs_out, 0, 0, sem).wait()
        return c

    lax.fori_loop(0, tb * TOP_K, drain, 0)


def _dispatch(dest_flat, vp, n_rows):
    t, w = vp.shape
    tb = DISPATCH_TOKENS
    xs0 = jnp.zeros((n_rows, w), U32)
    return pl.pallas_call(
        functools.partial(_dispatch_kernel, tb=tb),
        grid=(t // tb,),
        in_specs=[pl.BlockSpec((tb * TOP_K,), lambda i: (i,), memory_space=pltpu.SMEM),
                  pl.BlockSpec(memory_space=pl.ANY),
                  pl.BlockSpec(memory_space=pl.ANY)],
        out_specs=pl.BlockSpec(memory_space=pl.ANY),
        out_shape=jax.ShapeDtypeStruct((n_rows, w), U32),
        scratch_shapes=[pltpu.SemaphoreType.DMA(())],
        input_output_aliases={2: 0},
        compiler_params=_params("arbitrary"),
    )(dest_flat, vp, xs0)


def _expert_kernel(be_ref, bn_ref, xs_ref, win_ref, bin_ref, wout_ref, bout_ref, o_ref):
    del be_ref
    i = pl.program_id(0)
    dff = wout_ref.shape[1]

    @pl.when(bn_ref[i] > 0)
    def _():
        hi, lo = _unpack_bf16_pairs(xs_ref[...])
        x = jnp.concatenate([hi, lo], axis=-1).astype(BF16)
        h = jnp.dot(x, win_ref[0], preferred_element_type=F32) + bin_ref[0]
        gate = jnp.minimum(h[:, :dff], SWIGLU_LIMIT)
        up = jnp.clip(h[:, dff:], -SWIGLU_LIMIT, SWIGLU_LIMIT)
        glu = gate * jax.nn.sigmoid(SWIGLU_ALPHA * gate)
        act = ((up + 1.0) * glu).astype(BF16)
        out = jnp.dot(act, wout_ref[0], preferred_element_type=F32) + bout_ref[0]
        o_ref[...] = _pack_bf16_pairs(out)

    @pl.when(bn_ref[i] <= 0)
    def _():
        o_ref[...] = jnp.zeros_like(o_ref)


def _experts(block_e, block_n, xs, w_in, b_in, w_out, b_out):
    n_rows, w = xs.shape
    rb = EXPERT_ROWS
    ne, d, dff2 = w_in.shape
    dff = w_out.shape[1]
    return pl.pallas_call(
        _expert_kernel,
        grid_spec=pltpu.PrefetchScalarGridSpec(
            num_scalar_prefetch=2,
            grid=(n_rows // rb,),
            in_specs=[pl.BlockSpec((rb, w), lambda i, be, bn: (i, 0)),
                      pl.BlockSpec((1, d, dff2), lambda i, be, bn: (be[i], 0, 0)),
                      pl.BlockSpec((1, 1, dff2), lambda i, be, bn: (be[i], 0, 0)),
                      pl.BlockSpec((1, dff, d), lambda i, be, bn: (be[i], 0, 0)),
                      pl.BlockSpec((1, 1, d), lambda i, be, bn: (be[i], 0, 0))],
            out_specs=pl.BlockSpec((rb, w), lambda i, be, bn: (i, 0))),
        out_shape=jax.ShapeDtypeStruct((n_rows, w), U32),
        compiler_params=_params("arbitrary"),
    )(block_e, block_n, xs, w_in, b_in.reshape(ne, 1, dff2), w_out, b_out.reshape(ne, 1, d))


def _combine_kernel(tb_ref, dest_ref, outp_hbm, rw_ref, x1_ref, mod_ref, y_ref, buf, sem, *, tc):
    del tb_ref

    def issue(j, c):
        for kk in range(TOP_K):
            _row_copy(outp_hbm, buf, dest_ref[j * TOP_K + kk], kk * tc + j, sem).start()
        return c

    lax.fori_loop(0, tc, issue, 0)

    def drain(j, c):
        _row_copy(outp_hbm, buf, 0, 0, sem).wait()
        return c

    lax.fori_loop(0, tc * TOP_K, drain, 0)
    rw = rw_ref[...]
    acc = None
    for kk in range(TOP_K):
        hi, lo = _unpack_bf16_pairs(buf[kk * tc:(kk + 1) * tc, :])
        term = rw[:, kk:kk + 1] * jnp.concatenate([hi, lo], axis=-1)
        acc = term if acc is None else acc + term
    gate_f = mod_ref[0, 5:6, :]
    y_ref[...] = x1_ref[...] + gate_f * acc


def _combine(tile_b, dest_flat, outp, rw, x1, mod3, tok_off, n_tok):
    d = x1.shape[1]
    w = outp.shape[1]
    tc = COMBINE_TOKENS
    boff = tok_off // tc
    row = lambda i, tb: (boff + i, 0)
    return pl.pallas_call(
        functools.partial(_combine_kernel, tc=tc),
        grid_spec=pltpu.PrefetchScalarGridSpec(
            num_scalar_prefetch=1,
            grid=(n_tok // tc,),
            in_specs=[pl.BlockSpec((tc * TOP_K,), lambda i, tb: (boff + i,), memory_space=pltpu.SMEM),
                      pl.BlockSpec(memory_space=pl.ANY),
                      pl.BlockSpec((tc, LANES), row),
                      pl.BlockSpec((tc, d), row),
                      pl.BlockSpec((1, N_MOD, d), lambda i, tb: (tb[boff + i], 0, 0))],
            out_specs=pl.BlockSpec((tc, d), lambda i, tb: (i, 0)),
            scratch_shapes=[pltpu.VMEM((TOP_K * tc, w), U32), pltpu.SemaphoreType.DMA(())]),
        out_shape=jax.ShapeDtypeStruct((n_tok, d), F32),
        compiler_params=_params("arbitrary"),
    )(tile_b, dest_flat, outp, rw, x1, mod3)


def _head_slots(w, per_head, offset=0):
    k = w.shape[0]
    w = w.reshape(k, HEADS, per_head)
    w = jnp.pad(w, ((0, 0), (0, 0), (offset, LANES - per_head - offset)))
    return w.reshape(k, HEADS * LANES)


def _tile_table(seq_groups, tile, value_fn):
    vals = []
    for gi, (nb, s_len, off) in enumerate(seq_groups):
        for b in range(nb):
            for j in range(s_len // tile):
                vals.append(value_fn(gi, b, j))
    return jnp.asarray(np.asarray(vals, np.int32))


def _layer(x, mod3, seq_groups, batch_base, p):
    t, d = x.shape
    c5 = p["s5_d"].shape[0]
    q_lora, kv_lora = p["g_q_lat"].shape[0], p["g_kv_lat"].shape[0]
    ne = p["w_router"].shape[1]
    max_s = max(s for _, s, _ in seq_groups)
    n_steps = max(1, int(math.ceil(math.log2(max_s // LANES))))

    tile_b = _tile_table(seq_groups, COMBINE_TOKENS, lambda gi, b, j: batch_base[gi] + b)
    tile_b_tok = tile_b[::TOKEN_TILE // COMBINE_TOKENS]
    tile_pb = _tile_table(seq_groups, QKV_TILE, lambda gi, b, j: j)
    pos = []
    for nb, s_len, _ in seq_groups:
        nc = s_len // LANES
        for _ in range(nb):
            pos.append(np.stack([np.arange(nc), nc - 1 - np.arange(nc)], axis=1))
    chunk_pos = jnp.asarray(np.concatenate(pos, axis=0).astype(np.int32))

    w_in = p["w_in"]
    o1, o2, o3 = c5, c5 + q_lora, c5 + q_lora + kv_lora
    ws5t = w_in[:, :o1].T.astype(BF16)
    w_kr = jnp.pad(w_in[:, o3:], ((0, 0), (QK_NOPE, LANES - QK_HEAD)))
    wlat = jnp.concatenate([w_in[:, o1:o3], w_kr], axis=1).astype(BF16)
    xs5t, lat, gates = _pre_mixer(x, mod3, tile_b_tok, p["g_mix"].reshape(1, d), ws5t, wlat,
                                  p["w_gate"].astype(BF16), p["b_gate"].reshape(1, -1))

    tables = _s5_tables(p["s5_lam_re"], p["s5_lam_im"], p["s5_log_dt"], p["s5_b_re"], p["s5_b_im"],
                        p["s5_c_re"], p["s5_c_im"], p["s5_d"], n_steps)
    yt = _s5_mixer(xs5t, chunk_pos, tables, n_steps)

    wq = _head_slots(p["w_q_up"], QK_HEAD).astype(BF16)
    wkv = p["w_kv_up"].reshape(kv_lora, HEADS, QK_NOPE + V_HEAD)
    wk = _head_slots(wkv[:, :, :QK_NOPE].reshape(kv_lora, -1), QK_NOPE).astype(BF16)
    wv3 = wkv[:, :, QK_NOPE:]
    wv_even = jnp.pad(wv3, ((0, 0), (0, 0), (0, LANES - V_HEAD)))
    wv_odd = jnp.pad(wv3, ((0, 0), (0, 0), (LANES - V_HEAD, 0)))
    is_odd = (jnp.arange(HEADS) % 2 == 1)[None, :, None]
    wv = jnp.where(is_odd, wv_odd, wv_even).reshape(kv_lora, HEADS * LANES).astype(BF16)
    one_lane = np.zeros((HEADS, LANES), np.float32)
    one_lane[0::2, V_HEAD] = 1.0
    one_lane[1::2, 0] = 1.0
    vone = jnp.asarray(one_lane.reshape(1, HEADS * LANES))
    gqn = jnp.pad(p["g_qnorm"], (0, LANES - QK_HEAD)).reshape(1, LANES)
    gkn = jnp.pad(p["g_knorm"], (0, LANES - QK_HEAD)).reshape(1, LANES)
    half = QK_ROPE // 2
    inv = ROPE_BASE ** (-jnp.arange(half, dtype=F32) / half)
    ang = jnp.arange(max_s, dtype=F32)[:, None] * inv[None, :]
    cs, sn = jnp.cos(ang), jnp.sin(ang)
    zeros = lambda n: jnp.zeros((max_s, n), F32)
    cos_t = jnp.concatenate([jnp.ones((max_s, QK_NOPE), F32), cs, cs, zeros(LANES - QK_HEAD)], axis=1)
    s1_t = jnp.concatenate([zeros(QK_NOPE), -sn, zeros(LANES - QK_NOPE - half)], axis=1)
    s2_t = jnp.concatenate([zeros(QK_NOPE + half), sn, zeros(LANES - QK_HEAD)], axis=1)
    qh, kh, vh = _qkv_prep(lat, tile_pb, p["g_q_lat"].reshape(1, -1), p["g_kv_lat"].reshape(1, -1),
                           wq, wk, wv, vone, gqn, gkn, cos_t, s1_t, s2_t)
    o = _attention(qh, kh, vh, seq_groups)

    wr = jnp.pad(p["w_router"], ((0, 0), (0, LANES - ne))).astype(BF16)
    br = jnp.concatenate([p["b_router"].astype(F32), jnp.full((LANES - ne,), NEG_BIG, F32)]).reshape(1, LANES)
    x1, vp, ri, rw, cnt = _mix_route(
        tile_b_tok, yt, o, gates, x, mod3, p["s5_w_glu"].T.astype(BF16), p["s5_b_glu"].reshape(c5, 1),
        p["w_br_s5"].astype(BF16), p["w_br_mla"].astype(BF16), p["w_out"].astype(BF16),
        p["g_ffn"].reshape(1, d), wr, br)

    rb = EXPERT_ROWS
    counts = cnt[0, :ne].astype(I32)
    padded = (counts + rb - 1) // rb * rb
    pad_end = jnp.cumsum(padded)
    pad_start = pad_end - padded
    top_e = ri[:, :TOP_K]
    dest = (pad_start[top_e] + ri[:, TOP_K:2 * TOP_K]).reshape(-1)
    n_rows = t * TOP_K + ne * rb
    n_blocks = n_rows // rb
    blk_row = jnp.arange(n_blocks, dtype=I32) * rb
    block_e = jnp.minimum(jnp.searchsorted(pad_end, blk_row, side="right"), ne - 1).astype(I32)
    block_n = jnp.clip(counts[block_e] - (blk_row - pad_start[block_e]), 0, rb).astype(I32)
    xs = _dispatch(dest, vp, n_rows)
    outp = _experts(block_e, block_n, xs, p["w_exp_in"].astype(BF16), p["b_exp_in"],
                    p["w_exp_out"].astype(BF16), p["b_exp_out"])
    ys = []
    for nb, s_len, off in seq_groups:
        y = _combine(tile_b, dest, outp, rw, x1, mod3, off, nb * s_len)
        ys.append(y.reshape(nb, s_len, d))
    return ys


def kernel(x_prompt, x_sample, c_prompt, c_sample, w_ada, b_ada, g_mix, g_ffn, w_in, s5_lam_re, s5_lam_im,
           s5_log_dt, s5_b_re, s5_b_im, s5_c_re, s5_c_im, s5_d, s5_w_glu, s5_b_glu, w_br_s5, g_q_lat, w_q_up,
           g_kv_lat, w_kv_up, g_qnorm, g_knorm, w_br_mla, w_gate, b_gate, w_out, w_router, b_router,
           w_exp_in, b_exp_in, w_exp_out, b_exp_out):
    names = ("w_ada", "b_ada", "g_mix", "g_ffn", "w_in", "s5_lam_re", "s5_lam_im", "s5_log_dt", "s5_b_re",
             "s5_b_im", "s5_c_re", "s5_c_im", "s5_d", "s5_w_glu", "s5_b_glu", "w_br_s5", "g_q_lat", "w_q_up",
             "g_kv_lat", "w_kv_up", "g_qnorm", "g_knorm", "w_br_mla", "w_gate", "b_gate", "w_out", "w_router",
             "b_router", "w_exp_in", "b_exp_in", "w_exp_out", "b_exp_out")
    stacked = (w_ada, b_ada, g_mix, g_ffn, w_in, s5_lam_re, s5_lam_im, s5_log_dt, s5_b_re, s5_b_im, s5_c_re,
               s5_c_im, s5_d, s5_w_glu, s5_b_glu, w_br_s5, g_q_lat, w_q_up, g_kv_lat, w_kv_up, g_qnorm,
               g_knorm, w_br_mla, w_gate, b_gate, w_out, w_router, b_router, w_exp_in, b_exp_in, w_exp_out,
               b_exp_out)
    bp, sp, d = x_prompt.shape
    bs, ss, _ = x_sample.shape
    seq_groups = ((bp, sp, 0), (bs, ss, bp * sp))
    batch_base = (0, bp)
    ys = [x_prompt, x_sample]
    c_all = jnp.concatenate([c_prompt, c_sample], axis=0)
    for l in range(w_ada.shape[0]):
        p = {n: a[l] for n, a in zip(names, stacked)}
        x = jnp.concatenate([ys[0].reshape(bp * sp, d), ys[1].reshape(bs * ss, d)], axis=0)
        mod3 = _modulation(c_all, p["w_ada"], p["b_ada"]).reshape(bp + bs, N_MOD, d)
        ys = _layer(x, mod3, seq_groups, batch_base, p)
    return (ys[0], ys[1])
```
